```python
import math
import jax, jax.numpy as jnp
from jax import lax
import numpy as np

D_MODEL = 1024
BATCH = 2
SEQ = 8192
DEPTH = 1
DEC_BATCH = 128
DEC_SEQ = 4
PAST_LEN = 8192
PAGE_SIZE = 128

H_A = 8
NOPE_DIM = 64
ROPE_DIM = 32
V_DIM = 64
Q_RANK = 384
KV_RANK = 256
H_B = 8
DH_B = 64
H_IDX = 8
D_IDX = 32
TOPK_MAX = 256
N_BUCKETS = 32
MAX_DISTANCE = 128
PEER_HEADS = 8
N_KEYS = 128
N_EXPERTS = N_KEYS * N_KEYS
PEER_DK = 256
PEER_TOPK = 16
PEER_CHUNK = 128
Q_BLOCK = 128
ROPE_THETA = 10000.0
EPS = 1e-6
MLA_SCALE = (NOPE_DIM + ROPE_DIM) ** -0.5
DSA_SCALE = DH_B ** -0.5
MIX_WIDTH = H_A * V_DIM + H_B * DH_B
IN_SPLITS = (Q_RANK, KV_RANK, ROPE_DIM, H_B * DH_B, H_B * DH_B, H_B * DH_B, H_IDX * D_IDX, D_IDX, H_IDX)
N_IN = sum(IN_SPLITS)

kernel_name = 'hymba_mla_dsa_peer_adaln_step'


def rmsnorm(x, g):
    xf = x.astype(jnp.float32)
    y = xf * lax.rsqrt(jnp.mean(xf * xf, axis=-1, keepdims=True) + EPS)
    return y.astype(x.dtype) * g


def rope(x, pos):
    half = x.shape[-1] // 2
    inv = ROPE_THETA ** (-jnp.arange(half, dtype=jnp.float32) / half)
    ang = pos.astype(jnp.float32)[..., None] * inv
    cos = jnp.cos(ang).astype(x.dtype)
    sin = jnp.sin(ang).astype(x.dtype)
    x1, x2 = x[..., :half], x[..., half:]
    return jnp.concatenate([x1 * cos - x2 * sin, x1 * sin + x2 * cos], axis=-1)


def t5_bucket(dist):
    n = jnp.maximum(dist, 0)
    max_exact = N_BUCKETS // 2
    nf = jnp.maximum(n, 1).astype(jnp.float32)
    large = max_exact + (jnp.log(nf / max_exact) / math.log(MAX_DISTANCE / max_exact)
                         * (N_BUCKETS - max_exact)).astype(jnp.int32)
    large = jnp.minimum(large, N_BUCKETS - 1)
    return jnp.where(n < max_exact, n, large)


def adaln_mods(c, w_ada, b_ada):
    m = jax.nn.silu(c) @ w_ada + b_ada
    return tuple(t[:, None, :] for t in jnp.split(m, 6, axis=-1))


def modulate(x, g, shift, scale):
    return rmsnorm(x, g) * (1.0 + scale) + shift


def token_projections(h, pos, w_in, g_cq, w_uq, g_qa, g_ckv, g_kr, w_uk, g_qb, g_kb):
    n, t, _ = h.shape
    offs = np.cumsum(IN_SPLITS)[:-1].tolist()
    cq, ckv, kr, qb, kb, vb, qi, ki, wi = jnp.split(h @ w_in, offs, axis=-1)
    qa = (rmsnorm(cq, g_cq) @ w_uq).reshape(n, t, H_A, NOPE_DIM + ROPE_DIM)
    qa = rmsnorm(qa, g_qa)
    q_nope = qa[..., :NOPE_DIM]
    q_rope = rope(qa[..., NOPE_DIM:], pos[:, None])
    q_lat = jnp.einsum('nthd,hdr->nthr', q_nope, w_uk)
    ckv = rmsnorm(ckv, g_ckv)
    kr = rope(rmsnorm(kr, g_kr), pos)
    qb = rmsnorm(qb.reshape(n, t, H_B, DH_B), g_qb)
    kb = rmsnorm(kb.reshape(n, t, H_B, DH_B), g_kb)
    vb = vb.reshape(n, t, H_B, DH_B)
    qi = qi.reshape(n, t, H_IDX, D_IDX)
    return q_lat, q_rope, ckv, kr, qb, kb, vb, qi, ki, wi


def mla_attend(q_lat, q_rope, kv_lat, k_rope, q_pos, k_pos):
    s = (jnp.einsum('qhr,lr->hql', q_lat, kv_lat).astype(jnp.float32)
         + jnp.einsum('qhp,lp->hql', q_rope, k_rope).astype(jnp.float32)) * MLA_SCALE
    s = jnp.where(k_pos[None, None, :] <= q_pos[None, :, None], s, -jnp.inf)
    p = jax.nn.softmax(s, axis=-1).astype(kv_lat.dtype)
    return jnp.einsum('hql,lr->qhr', p, kv_lat)


def indexer_topk(qi, wi, ki, q_pos, k_pos, k_sel):
    dots = jnp.einsum('...qhd,...ld->...qhl', qi, ki)
    score = jnp.einsum('...qh,...qhl->...ql', wi, jax.nn.relu(dots)).astype(jnp.float32)
    score = jnp.where(k_pos[None, :] <= q_pos[:, None], score, -jnp.inf)
    return lax.top_k(score, k_sel)[1]


def sparse_attend(q, k_sel, v_sel, q_pos, sel_pos, rel_bias):
    s = jnp.einsum('...qhd,...qkhd->...qhk', q, k_sel).astype(jnp.float32) * DSA_SCALE
    dist = q_pos[:, None] - sel_pos
    bias = jnp.moveaxis(rel_bias[t5_bucket(dist)], -1, -2).astype(jnp.float32)
    s = jnp.where((dist >= 0)[..., None, :], s + bias, -jnp.inf)
    p = jax.nn.softmax(s, axis=-1).astype(v_sel.dtype)
    return jnp.einsum('...qhk,...qkhd->...qhd', p, v_sel)


def prompt_mixers(q_lat, q_rope, ckv, kr, qb, kb, vb, qi, ki, wi, pos, k_sel, rel_bias):
    b, s = q_lat.shape[:2]
    nblk = s // Q_BLOCK

    def blocks(a):
        return a.reshape(b, nblk, Q_BLOCK, *a.shape[2:]).swapaxes(0, 1)

    def unblock(a):
        return a.swapaxes(0, 1).reshape(b, s, *a.shape[3:])

    take = jax.vmap(lambda a, i: a[i])

    def block_fn(args):
        ql, qr, qq, qix, wix, qp = args
        o_lat = jax.vmap(mla_attend, in_axes=(0, 0, 0, 0, None, None))(ql, qr, ckv, kr, qp, pos)
        idx = indexer_topk(qix, wix, ki, qp, pos, k_sel)
        o_b = sparse_attend(qq, take(kb, idx), take(vb, idx), qp, idx, rel_bias)
        return o_lat, o_b

    o_lat, o_b = lax.map(block_fn, (blocks(q_lat), blocks(q_rope), blocks(qb), blocks(qi),
                                    blocks(wi), pos.reshape(nblk, Q_BLOCK)))
    return unblock(o_lat), unblock(o_b)


def sample_mixers(q_lat, q_rope, ckv, kr, qb, kb, vb, qi, ki, wi, pos, k_sel,
                  c_lat, c_rope, c_dk, c_dv, c_ik, page_table, rel_bias):
    nb, t = q_lat.shape[:2]
    page = c_lat.shape[1]
    past = page_table.shape[1] * page
    k_pos = jnp.arange(past + t, dtype=jnp.int32)

    def seq_fn(args):
        pt, ql, qr, lat_new, rope_new = args
        kv_lat = jnp.concatenate([c_lat[pt].reshape(past, KV_RANK), lat_new], axis=0)
        k_rope = jnp.concatenate([c_rope[pt].reshape(past, ROPE_DIM), rope_new], axis=0)
        return mla_attend(ql, qr, kv_lat, k_rope, pos, k_pos)

    o_lat = lax.map(seq_fn, (page_table, q_lat, q_rope, ckv, kr))

    ki_all = jnp.concatenate([c_ik[page_table].reshape(nb, past, D_IDX), ki], axis=1)
    idx = indexer_topk(qi, wi, ki_all, pos, k_pos, k_sel)
    in_past = idx < past
    pidx = jnp.minimum(idx, past - 1)
    phys = jax.vmap(lambda p, i: p[i])(page_table, pidx // page)
    off = pidx % page
    nidx = jnp.clip(idx - past, 0, t - 1)
    take = jax.vmap(lambda a, i: a[i])

    def select(cache, new):
        return jnp.where(in_past[..., None, None], cache[phys, off], take(new, nidx))

    o_b = sparse_attend(qb, select(c_dk, kb), select(c_dv, vb), pos, idx, rel_bias)
    return o_lat, o_b


def peer_ffn(h, w_pq, subkeys, u, v):
    shp = h.shape
    flat = h.reshape(-1, shp[-1])
    n_tok = flat.shape[0]
    n_chunks = -(-n_tok // PEER_CHUNK)
    flat = jnp.pad(flat, ((0, n_chunks * PEER_CHUNK - n_tok), (0, 0))).reshape(n_chunks, PEER_CHUNK, shp[-1])

    def chunk(hc):
        c = hc.shape[0]
        q = (hc @ w_pq).reshape(c, PEER_HEADS, 2, PEER_DK // 2)
        s = jnp.einsum('chpd,hpnd->chpn', q, subkeys).astype(jnp.float32)
        s1, i1 = lax.top_k(s[:, :, 0], PEER_TOPK)
        s2, i2 = lax.top_k(s[:, :, 1], PEER_TOPK)
        cand = (s1[..., :, None] + s2[..., None, :]).reshape(c, PEER_HEADS, PEER_TOPK * PEER_TOPK)
        cidx = (i1[..., :, None] * N_KEYS + i2[..., None, :]).reshape(c, PEER_HEADS, PEER_TOPK * PEER_TOPK)
        top, sel = lax.top_k(cand, PEER_TOPK)
        eidx = jnp.take_along_axis(cidx, sel, axis=-1)
        g = jax.nn.softmax(top, axis=-1).astype(hc.dtype)
        act = jax.nn.gelu(jnp.einsum('cd,chkd->chk', hc, u[eidx]), approximate=False) * g
        return jnp.einsum('chk,chkd->cd', act, v[eidx])

    out = lax.map(chunk, flat).reshape(-1, shp[-1])[:n_tok]
    return out.reshape(shp)


def layer_tail(x, o_lat, o_b, mods, w_uv, w_out, g_norm2, w_pq, subkeys, u, v):
    _, _, gate1, shift2, scale2, gate2 = mods
    n, t = x.shape[:2]
    o_a = jnp.einsum('nthr,hrd->nthd', o_lat, w_uv)
    mixed = jnp.concatenate([o_a, o_b], axis=2).reshape(n, t, MIX_WIDTH)
    x = x + gate1 * (mixed @ w_out)
    h = modulate(x, g_norm2, shift2, scale2)
    return x + gate2 * peer_ffn(h, w_pq, subkeys, u, v)


def setup_inputs(seed: int = 0) -> dict:
    key = jax.random.key(seed)
    keys = iter(jax.random.split(key, 40))

    def nrm(shape, scale):
        return jax.random.normal(next(keys), shape, jnp.float32) * scale

    def gain(shape):
        return 1.0 + nrm(shape, 0.05)

    L = DEPTH
    n_pages = PAST_LEN // PAGE_SIZE
    n_used = DEC_BATCH * n_pages
    n_phys = n_used + (n_used + 3) // 4
    perm = jax.random.permutation(next(keys), n_phys)
    page_table = perm[:n_used].reshape(DEC_BATCH, n_pages).astype(jnp.int32)
    pool = (L, n_phys, PAGE_SIZE)
    return {
        'x_prompt': nrm((BATCH, SEQ, D_MODEL), 1.0),
        'x_sample': nrm((DEC_BATCH, DEC_SEQ, D_MODEL), 1.0),
        'cache_mla_latent': nrm(pool + (KV_RANK,), 1.0),
        'cache_mla_rope': nrm(pool + (ROPE_DIM,), 1.0),
        'cache_dsa_k': nrm(pool + (H_B, DH_B), 1.0),
        'cache_dsa_v': nrm(pool + (H_B, DH_B), 1.0),
        'cache_idx_k': nrm(pool + (D_IDX,), 1.0),
        'page_table': page_table,
        'c_prompt': nrm((BATCH, D_MODEL), 1.0),
        'c_sample': nrm((DEC_BATCH, D_MODEL), 1.0),
        'w_ada': nrm((L, D_MODEL, 6 * D_MODEL), 0.5 * D_MODEL ** -0.5),
        'b_ada': nrm((L, 6 * D_MODEL), 0.02),
        'g_norm1': gain((L, D_MODEL)),
        'g_norm2': gain((L, D_MODEL)),
        'w_in': nrm((L, D_MODEL, N_IN), D_MODEL ** -0.5),
        'g_cq': gain((L, Q_RANK)),
        'w_uq': nrm((L, Q_RANK, H_A * (NOPE_DIM + ROPE_DIM)), Q_RANK ** -0.5),
        'g_qa': gain((L, NOPE_DIM + ROPE_DIM)),
        'g_ckv': gain((L, KV_RANK)),
        'g_kr': gain((L, ROPE_DIM)),
        'w_uk': nrm((L, H_A, NOPE_DIM, KV_RANK), KV_RANK ** -0.5),
        'w_uv': nrm((L, H_A, KV_RANK, V_DIM), KV_RANK ** -0.5),
        'g_qb': gain((L, DH_B)),
        'g_kb': gain((L, DH_B)),
        'rel_bias': nrm((N_BUCKETS, H_B), 0.5),
        'w_out': nrm((L, MIX_WIDTH, D_MODEL), MIX_WIDTH ** -0.5),
        'w_pq': nrm((L, D_MODEL, PEER_HEADS * PEER_DK), D_MODEL ** -0.5),
        'peer_subkeys': nrm((L, PEER_HEADS, 2, N_KEYS, PEER_DK // 2), (PEER_DK // 2) ** -0.5),
        'peer_u': nrm((L, N_EXPERTS, D_MODEL), D_MODEL ** -0.5),
        'peer_v': nrm((L, N_EXPERTS, D_MODEL), PEER_HEADS ** -0.5),
    }


def reference(x_prompt, x_sample, cache_mla_latent, cache_mla_rope, cache_dsa_k, cache_dsa_v,
              cache_idx_k, page_table, c_prompt, c_sample, w_ada, b_ada, g_norm1, g_norm2, w_in,
              g_cq, w_uq, g_qa, g_ckv, g_kr, w_uk, w_uv, g_qb, g_kb, rel_bias, w_out, w_pq,
              peer_subkeys, peer_u, peer_v):
    seq = x_prompt.shape[1]
    t_new = x_sample.shape[1]
    past = page_table.shape[1] * cache_mla_latent.shape[2]
    pos_p = jnp.arange(seq, dtype=jnp.int32)
    pos_s = past + jnp.arange(t_new, dtype=jnp.int32)
    k_p = min(TOPK_MAX, seq // 4)
    k_s = min(TOPK_MAX, (past + t_new) // 4)
    xp, xs = x_prompt, x_sample
    rows_p = [[] for _ in range(5)]
    rows_s = [[] for _ in range(5)]
    for l in range(DEPTH):
        proj_w = (w_in[l], g_cq[l], w_uq[l], g_qa[l], g_ckv[l], g_kr[l], w_uk[l], g_qb[l], g_kb[l])
        tail_w = (w_uv[l], w_out[l], g_norm2[l], w_pq[l], peer_subkeys[l], peer_u[l], peer_v[l])
        mods_p = adaln_mods(c_prompt, w_ada[l], b_ada[l])
        mods_s = adaln_mods(c_sample, w_ada[l], b_ada[l])
        pp = token_projections(modulate(xp, g_norm1[l], mods_p[0], mods_p[1]), pos_p, *proj_w)
        ps = token_projections(modulate(xs, g_norm1[l], mods_s[0], mods_s[1]), pos_s, *proj_w)
        oa_p, ob_p = prompt_mixers(*pp, pos_p, k_p, rel_bias)
        oa_s, ob_s = sample_mixers(*ps, pos_s, k_s, cache_mla_latent[l], cache_mla_rope[l],
                                   cache_dsa_k[l], cache_dsa_v[l], cache_idx_k[l], page_table, rel_bias)
        xp = layer_tail(xp, oa_p, ob_p, mods_p, *tail_w)
        xs = layer_tail(xs, oa_s, ob_s, mods_s, *tail_w)
        for j, src in enumerate((2, 3, 5, 6, 8)):
            rows_p[j].append(pp[src])
            rows_s[j].append(ps[src])
    lat_p, rope_p, dsa_k_p, dsa_v_p, idx_k_p = [jnp.stack(r) for r in rows_p]
    lat_s, rope_s, dsa_k_s, dsa_v_s, idx_k_s = [jnp.stack(r) for r in rows_s]
    return (xp, xs, lat_p, rope_p, dsa_k_p, dsa_v_p, idx_k_p, lat_s, rope_s, dsa_k_s, dsa_v_s, idx_k_s)
```

```python
import functools
import math

import numpy as np
import jax
import jax.numpy as jnp
from jax import lax
from jax.experimental import pallas as pl
from jax.experimental.pallas import tpu as pltpu

H_A = 8
NOPE_DIM = 64
ROPE_DIM = 32
V_DIM = 64
Q_RANK = 384
KV_RANK = 256
H_B = 8
DH_B = 64
H_IDX = 8
D_IDX = 32
TOPK_MAX = 256
N_BUCKETS = 32
MAX_DISTANCE = 128
PEER_HEADS = 8
N_KEYS = 128
PEER_DK = 256
PEER_TOPK = 16
ROPE_THETA = 10000.0
EPS = 1e-6
MLA_SCALE = (NOPE_DIM + ROPE_DIM) ** -0.5
DSA_SCALE = DH_B ** -0.5
IN_SPLITS = (Q_RANK, KV_RANK, ROPE_DIM, H_B * DH_B, H_B * DH_B, H_B * DH_B, H_IDX * D_IDX, D_IDX, H_IDX)

LANES = 128
SUBLANES = 8
VMEM_LIMIT = 56 * 1024 * 1024
NEG_INF = float("-inf")
M_INIT = -1e30

C_CQ = 0
C_CKV = C_CQ + Q_RANK
C_QB = C_CKV + KV_RANK
C_KB = C_QB + H_B * LANES
C_VB = C_KB + H_B * DH_B
C_QI = C_VB + H_B * DH_B
C_MISC = C_QI + H_IDX * D_IDX
C_KI4 = C_MISC + LANES
N_INP = C_KI4 + LANES
M_KI = 0
M_WI = D_IDX
M_KR = NOPE_DIM

F32 = jnp.float32
BF16 = jnp.bfloat16


def _nt(a, b):
    return lax.dot_general(a, b, (((1,), (1,)), ((), ())), preferred_element_type=F32)


def _nn(a, b):
    return jnp.dot(a, b, preferred_element_type=F32)


def _cparams(sem):
    return pltpu.CompilerParams(dimension_semantics=sem, vmem_limit_bytes=VMEM_LIMIT)


def _const_spec(shape):
    nd = len(shape)
    return pl.BlockSpec(shape, lambda *_: (0,) * nd)


def _ada_body(c_ref, w_ref, b_ref, o_ref):
    c = c_ref[...]
    a = c / (1.0 + jnp.exp(-c))
    o_ref[...] = jnp.dot(a, w_ref[...], preferred_element_type=F32,
                         precision=lax.Precision.HIGHEST) + b_ref[...]


def _ada_mods(c, w_ada, b_ada):
    n, d = c.shape
    n_out = w_ada.shape[1]
    tn = 768
    return pl.pallas_call(
        _ada_body,
        out_shape=jax.ShapeDtypeStruct((n, n_out), F32),
        grid=(n_out // tn,),
        in_specs=[pl.BlockSpec((n, d), lambda j: (0, 0)),
                  pl.BlockSpec((d, tn), lambda j: (0, j)),
                  pl.BlockSpec((1, tn), lambda j: (0, j))],
        out_specs=pl.BlockSpec((n, tn), lambda j: (0, j)),
        compiler_params=_cparams(("arbitrary",)),
        name="ada_mods",
    )(c, w_ada, b_ada.reshape(1, n_out))


def _rms(x, n):
    return x * lax.rsqrt(jnp.sum(x * x, axis=-1, keepdims=True) * (1.0 / n) + EPS)


def _rot(y, tc, ts1, ts2):
    half = ROPE_DIM // 2
    return y * tc + pltpu.roll(y, LANES - half, 1) * ts1 + pltpu.roll(y, half, 1) * ts2


def _front_body(x_ref, shift_ref, scale_ref, tc_ref, ts1_ref, ts2_ref, g1_ref, win_ref, gcq_ref,
                wuq_ref, gqa_ref, gckv_ref, gkr_ref, wk_ref, rpl_ref, wv_ref, gqb_ref, gkb_ref,
                lat_ref, kbn_ref, vbf_ref, misc_ref, qcat_ref, kcat_ref, vcat_ref, qbp_ref,
                kb16_ref, vb16_ref, qi_ref, ki4_ref):
    d = x_ref.shape[-1]
    x = x_ref[...]
    h = _rms(x, d) * g1_ref[...] * (1.0 + scale_ref[...]) + shift_ref[...]
    p = _nn(h.astype(BF16), win_ref[...])
    tc, ts1, ts2 = tc_ref[...], ts1_ref[...], ts2_ref[...]
    lane = lax.broadcasted_iota(jnp.int32, (1, LANES), 1)

    cq = _rms(p[:, C_CQ:C_CQ + Q_RANK], Q_RANK) * gcq_ref[...]
    qa = _nn(cq.astype(BF16), wuq_ref[...])
    for hh in range(H_A):
        blk = _rms(qa[:, hh * LANES:(hh + 1) * LANES], NOPE_DIM + ROPE_DIM) * gqa_ref[...]
        qcat_ref[:, hh * LANES:(hh + 1) * LANES] = (_rot(blk, tc, ts1, ts2) * MLA_SCALE).astype(BF16)

    ckv = _rms(p[:, C_CKV:C_CKV + KV_RANK], KV_RANK) * gckv_ref[...]
    lat_ref[...] = ckv
    misc = p[:, C_MISC:C_MISC + LANES]
    kr_mask = (lane >= M_KR) & (lane < M_KR + ROPE_DIM)
    krs = jnp.where(kr_mask, misc, 0.0)
    krn = _rms(krs, ROPE_DIM) * gkr_ref[...]
    kr_rot = _rot(krn, tc, ts1, ts2)
    misc_ref[...] = jnp.where(kr_mask, kr_rot, misc)
    ckv16 = ckv.astype(BF16)
    kcat_ref[...] = (_nn(ckv16, wk_ref[...]) + _nn(kr_rot.astype(BF16), rpl_ref[...])).astype(BF16)
    vcat_ref[...] = _nn(ckv16, wv_ref[...]).astype(BF16)

    for hh in range(H_B):
        blk = _rms(p[:, C_QB + hh * LANES:C_QB + (hh + 1) * LANES], DH_B) * gqb_ref[:, hh * LANES:(hh + 1) * LANES]
        qbp_ref[:, hh * LANES:(hh + 1) * LANES] = (blk * DSA_SCALE).astype(BF16)
    lo = lane < DH_B
    for j in range(H_B * DH_B // LANES):
        blk = p[:, C_KB + j * LANES:C_KB + (j + 1) * LANES]
        sq = blk * blk
        s_lo = jnp.sum(jnp.where(lo, sq, 0.0), axis=-1, keepdims=True)
        s_hi = jnp.sum(jnp.where(lo, 0.0, sq), axis=-1, keepdims=True)
        r = lax.rsqrt(jnp.where(lo, s_lo, s_hi) * (1.0 / DH_B) + EPS)
        kn = blk * r * gkb_ref[:, j * LANES:(j + 1) * LANES]
        kbn_ref[:, j * LANES:(j + 1) * LANES] = kn
        kb16_ref[:, j * LANES:(j + 1) * LANES] = kn.astype(BF16)
    vb = p[:, C_VB:C_VB + H_B * DH_B]
    vbf_ref[...] = vb
    vb16_ref[...] = vb.astype(BF16)
    qi_ref[...] = p[:, C_QI:C_QI + H_IDX * D_IDX].astype(BF16)
    ki4_ref[...] = p[:, C_KI4:C_KI4 + LANES].astype(BF16)


def _front(x2, shift3, scale3, mod_map, tabs, tab_map, w, t):
    n, d = x2.shape
    r = shift3.shape[1]
    row = lambda i: (i, 0)
    mod_spec = pl.BlockSpec((None, r, d), mod_map)
    tab_spec = pl.BlockSpec((t, LANES), tab_map)
    weights = [w["g1"], w["win"], w["gcq"], w["wuq"], w["gqa"], w["gckv"], w["gkr"], w["wk"],
               w["rpl"], w["wv"], w["gqb"], w["gkb"]]
    widths = [(KV_RANK, F32), (H_B * DH_B, F32), (H_B * DH_B, F32), (LANES, F32),
              (H_A * LANES, BF16), (H_A * LANES, BF16), (H_A * V_DIM, BF16), (H_B * LANES, BF16),
              (H_B * DH_B, BF16), (H_B * DH_B, BF16), (H_IDX * D_IDX, BF16), (LANES, BF16)]
    return pl.pallas_call(
        _front_body,
        out_shape=[jax.ShapeDtypeStruct((n, c), dt) for c, dt in widths],
        grid=(n // t,),
        in_specs=[pl.BlockSpec((t, d), row), mod_spec, mod_spec, tab_spec, tab_spec, tab_spec]
                 + [_const_spec(a.shape) for a in weights],
        out_specs=[pl.BlockSpec((t, c), row) for c, _ in widths],
        compiler_params=_cparams(("arbitrary",)),
        name="front",
    )(x2, shift3, scale3, *tabs, *weights)


def _mla_body(q_ref, k_ref, v_ref, o_ref, m_sc, l_sc, acc_sc, *, tq):
    qt = pl.program_id(2)
    row = lax.broadcasted_iota(jnp.int32, (tq, tq), 0)
    col = lax.broadcasted_iota(jnp.int32, (tq, tq), 1)
    outs = []
    for hh in range(2):
        q = q_ref[:, hh * LANES:(hh + 1) * LANES]
        m_sc[...] = jnp.full(m_sc.shape, M_INIT, F32)
        l_sc[...] = jnp.zeros(l_sc.shape, F32)
        acc_sc[...] = jnp.zeros(acc_sc.shape, F32)

        def step(c, diag, q=q, hh=hh):
            start = pl.multiple_of(c * tq, tq)
            s = _nt(q, k_ref[pl.ds(start, tq), hh * LANES:(hh + 1) * LANES])
            if diag:
                s = jnp.where(col <= row, s, NEG_INF)
            m_old = m_sc[...]
            m_new = jnp.maximum(m_old, jnp.max(s, axis=-1, keepdims=True))
            alpha = jnp.exp(m_old - m_new)
            pr = jnp.exp(s - m_new)
            l_sc[...] = alpha * l_sc[...] + jnp.sum(pr, axis=-1, keepdims=True)
            acc_sc[...] = alpha * acc_sc[...] + _nn(pr.astype(BF16), v_ref[pl.ds(start, tq), :])
            m_sc[...] = m_new

        def loop_body(c, carry):
            step(c, False)
            return carry

        lax.fori_loop(0, qt, loop_body, 0)
        step(qt, True)
        outs.append(acc_sc[...] / l_sc[...])
    lane = lax.broadcasted_iota(jnp.int32, (1, LANES), 1)
    o_ref[...] = jnp.where(lane < V_DIM, outs[0], outs[1])


def _mla_prompt(qcat, kcat, vcat, b, s, tq):
    n = qcat.shape[0]
    nqt = s // tq
    return pl.pallas_call(
        functools.partial(_mla_body, tq=tq),
        out_shape=jax.ShapeDtypeStruct((n, H_A * V_DIM), F32),
        grid=(b, H_A // 2, nqt),
        in_specs=[pl.BlockSpec((tq, 2 * LANES), lambda bi, hp, qt: (bi * nqt + qt, hp)),
                  pl.BlockSpec((s, 2 * LANES), lambda bi, hp, qt: (bi, hp)),
                  pl.BlockSpec((s, LANES), lambda bi, hp, qt: (bi, hp))],
        out_specs=pl.BlockSpec((tq, LANES), lambda bi, hp, qt: (bi * nqt + qt, hp)),
        scratch_shapes=[pltpu.VMEM((tq, 1), F32), pltpu.VMEM((tq, 1), F32), pltpu.VMEM((tq, LANES), F32)],
        compiler_params=_cparams(("arbitrary", "arbitrary", "arbitrary")),
        name="mla_prompt",
    )(qcat, kcat, vcat)


KEY_NEG_INF = -2139095041
INT_MIN = -2147483648
SEL_ROWS = 64


def _sort_key(s):
    b = pltpu.bitcast(s, jnp.int32)
    return b ^ ((b >> 31) & 0x7FFFFFFF)


def _select_topk(key_ref, n_ch, k):
    _, r_tot, w = key_ref.shape
    nfold = w // LANES
    neg_bits = jnp.int32(-8388608)
    sel_rows = min(SEL_ROWS, r_tot)
    k = float(k)

    def fold(x):
        acc = x[:, 0:LANES]
        for f in range(1, nfold):
            acc = acc + x[:, f * LANES:(f + 1) * LANES]
        return acc

    for rb in range(r_tot // sel_rows):
        rows = pl.ds(rb * sel_rows, sel_rows)

        def count(pred, rows=rows):
            def body(c, acc):
                return acc + fold(jnp.where(pred(key_ref[c, rows, :], c), 1.0, 0.0))
            acc = lax.fori_loop(0, n_ch, body, jnp.zeros((sel_rows, LANES), F32))
            return jnp.sum(acc, axis=-1, keepdims=True)

        def count_ge(t):
            return count(lambda x, c: x >= t)

        t0 = jnp.where(count_ge(jnp.zeros((sel_rows, 1), jnp.int32)) >= k, 0, INT_MIN).astype(jnp.int32)

        def bit_body(i, t):
            cand = t | (jnp.int32(1) << (30 - i))
            return jnp.where(count_ge(cand) >= k, cand, t)

        t = lax.fori_loop(0, 31, bit_body, t0)
        t = jnp.maximum(t, KEY_NEG_INF + 1)
        n_ge = count_ge(t)
        has_tie = jnp.max(n_ge) > k

        def write_plain():
            def body(c, carry):
                x = key_ref[c, rows, :]
                key_ref[c, rows, :] = jnp.where(x >= t, 0, neg_bits)
                return carry
            lax.fori_loop(0, n_ch, body, 0)

        def write_ties():
            need = k - count(lambda x, c: x > t)
            col = lax.broadcasted_iota(jnp.int32, (sel_rows, w), 1)
            nbits = max(1, int(math.ceil(math.log2(key_ref.shape[0] * w))))

            def cut_body(i, cut):
                cand = cut | (jnp.int32(1) << (nbits - 1 - i))
                below = count(lambda x, c: (x == t) & (col + c * w < cand))
                return jnp.where(below < need, cand, cut)

            cut = lax.fori_loop(0, nbits, cut_body, jnp.zeros((sel_rows, 1), jnp.int32))

            def body(c, carry):
                x = key_ref[c, rows, :]
                sel = (x > t) | ((x == t) & (col + c * w <= cut))
                key_ref[c, rows, :] = jnp.where(sel, 0, neg_bits)
                return carry
            lax.fori_loop(0, n_ch, body, 0)

        lax.cond(has_tie, write_ties, write_plain)


def _dsa_body(q_ref, k_ref, v_ref, qi_ref, ki_ref, misc_ref, dl_ref, o_ref, key_sc, wb_sc, m_sc, l_sc,
              acc_sc, *, t, k_sel):
    qt = pl.program_id(1)
    h = pl.program_id(2)
    n_ch = qt + 1
    row = lax.broadcasted_iota(jnp.int32, (t, t), 0)
    col = lax.broadcasted_iota(jnp.int32, (t, t), 1)
    lane = lax.broadcasted_iota(jnp.int32, (1, LANES), 1)

    @pl.when(h == 0)
    def _():
        wi = misc_ref[:, M_WI:M_WI + H_IDX]
        for hh in range(H_IDX):
            wb_sc[hh] = jnp.broadcast_to(wi[:, hh:hh + 1], (t, t))
        qs = []
        for hh in range(H_IDX):
            blk = qi_ref[:, (hh // 4) * LANES:(hh // 4 + 1) * LANES]
            lo = (hh % 4) * D_IDX
            qs.append(jnp.where((lane >= lo) & (lane < lo + D_IDX), blk, jnp.zeros_like(blk)))

        def score(c, diag):
            kc = ki_ref[pl.ds(pl.multiple_of(c * t, t), t), :]
            acc = jnp.zeros((t, t), F32)
            for hh in range(H_IDX):
                acc = acc + wb_sc[hh] * jnp.maximum(_nt(qs[hh], kc), 0.0)
            if diag:
                acc = jnp.where(col <= row, acc, NEG_INF)
            key_sc[c] = _sort_key(acc)

        def loop_body(c, carry):
            score(c, False)
            return carry

        lax.fori_loop(0, qt, loop_body, 0)
        score(qt, True)
        _select_topk(key_sc, n_ch, k_sel)

    m_sc[...] = jnp.full(m_sc.shape, M_INIT, F32)
    l_sc[...] = jnp.zeros(l_sc.shape, F32)
    acc_sc[...] = jnp.zeros(acc_sc.shape, F32)
    q = q_ref[...]

    def step(c, delta):
        start = pl.multiple_of(c * t, t)
        s = _nt(q, k_ref[pl.ds(start, t), :]) + pltpu.bitcast(key_sc[c], F32)
        if delta is not None:
            s = s + dl_ref[delta]
        m_old = m_sc[...]
        m_new = jnp.maximum(m_old, jnp.max(s, axis=-1, keepdims=True))
        alpha = jnp.exp(m_old - m_new)
        pr = jnp.exp(s - m_new)
        l_sc[...] = alpha * l_sc[...] + jnp.sum(pr, axis=-1, keepdims=True)
        acc_sc[...] = alpha * acc_sc[...] + _nn(pr.astype(BF16), v_ref[pl.ds(start, t), :])
        m_sc[...] = m_new

    def loop_body(c, carry):
        step(c, None)
        return carry

    lax.fori_loop(0, qt - 1, loop_body, 0)

    @pl.when(qt >= 1)
    def _():
        step(qt - 1, 1)

    step(qt, 0)
    res = acc_sc[...] / l_sc[...]

    @pl.when(h % 2 == 0)
    def _():
        o_ref[...] = res

    @pl.when(h % 2 == 1)
    def _():
        o_ref[...] = jnp.where(lane < DH_B, o_ref[...], res)


def _dsa_prompt(qbp, kb16, vb16, qi, ki4, misc, delta, b, s, t, k_sel):
    n = qbp.shape[0]
    nqt = s // t
    qrow = lambda bi, qt, h: (bi * nqt + qt, 0)
    return pl.pallas_call(
        functools.partial(_dsa_body, t=t, k_sel=k_sel),
        out_shape=jax.ShapeDtypeStruct((n, H_B * DH_B), F32),
        grid=(b, nqt, H_B),
        in_specs=[pl.BlockSpec((t, LANES), lambda bi, qt, h: (bi * nqt + qt, h)),
                  pl.BlockSpec((s, LANES), lambda bi, qt, h: (bi, h // 2)),
                  pl.BlockSpec((s, LANES), lambda bi, qt, h: (bi, h // 2)),
                  pl.BlockSpec((t, H_IDX * D_IDX), qrow),
                  pl.BlockSpec((s, LANES), lambda bi, qt, h: (bi, 0)),
                  pl.BlockSpec((t, LANES), qrow),
                  pl.BlockSpec((None, 2, t, t), lambda bi, qt, h: (h, 0, 0, 0))],
        out_specs=pl.BlockSpec((t, LANES), lambda bi, qt, h: (bi * nqt + qt, h // 2)),
        scratch_shapes=[pltpu.VMEM((nqt, t, t), jnp.int32), pltpu.VMEM((H_IDX, t, t), F32),
                        pltpu.VMEM((t, 1), F32), pltpu.VMEM((t, 1), F32), pltpu.VMEM((t, LANES), F32)],
        compiler_params=_cparams(("arbitrary", "arbitrary", "arbitrary")),
        name="dsa_prompt",
    )(qbp, kb16, vb16, qi, ki4, misc, delta)


def _qlat_body(q_ref, w_ref, o_ref):
    o_ref[...] = _nn(q_ref[...], w_ref[...]).astype(BF16)


def _sample_qlat(qcat_s, wukp):
    n = qcat_s.shape[0]
    return pl.pallas_call(
        _qlat_body,
        out_shape=jax.ShapeDtypeStruct((H_A, n, KV_RANK), BF16),
        grid=(H_A,),
        in_specs=[pl.BlockSpec((n, LANES), lambda h: (0, h)),
                  pl.BlockSpec((None, LANES, KV_RANK), lambda h: (h, 0, 0))],
        out_specs=pl.BlockSpec((None, n, KV_RANK), lambda h: (h, 0, 0)),
        compiler_params=_cparams(("arbitrary",)),
        name="sample_qlat",
    )(qcat_s, wukp)


def _pad_rows(x, rows):
    return jnp.concatenate([x, jnp.zeros((rows - x.shape[0],) + x.shape[1:], x.dtype)], axis=0)


def _smla_body(pt_ref, ql_ref, qr_ref, latn_ref, ropen_ref, wuv_ref, *rest, pg, page, tn):
    lat_refs = rest[:pg]
    rope_refs = rest[pg:2 * pg]
    o_ref, m_sc, l_sc, acc_sc = rest[2 * pg:]
    j = pl.program_id(1)
    nr = ql_ref.shape[0]
    ql = ql_ref[...]
    qr = qr_ref[...]

    @pl.when(j == 0)
    def _():
        m_sc[...] = jnp.full(m_sc.shape, M_INIT, F32)
        l_sc[...] = jnp.zeros(l_sc.shape, F32)
        acc_sc[...] = jnp.zeros(acc_sc.shape, F32)

    def update(s, vals):
        m_old = m_sc[...]
        m_new = jnp.maximum(m_old, jnp.max(s, axis=-1, keepdims=True))
        alpha = jnp.exp(m_old - m_new)
        pr = jnp.exp(s - m_new)
        l_sc[...] = alpha * l_sc[...] + jnp.sum(pr, axis=-1, keepdims=True)
        acc = alpha * acc_sc[...]
        for i, v in enumerate(vals):
            acc = acc + _nn(pr[:, i * page:(i + 1) * page].astype(BF16), v)
        acc_sc[...] = acc
        m_sc[...] = m_new

    lats = [r[...].astype(BF16) for r in lat_refs]
    ropes = [r[...].astype(BF16) for r in rope_refs]
    s = jnp.concatenate([_nt(ql, lats[i]) + _nt(qr, ropes[i]) for i in range(pg)], axis=1)
    update(s, lats)

    @pl.when(j == pl.num_programs(1) - 1)
    def _():
        latn = _pad_rows(latn_ref[...], page).astype(BF16)
        ropen = _pad_rows(ropen_ref[...], page).astype(BF16)
        sn = _nt(ql, latn) + _nt(qr, ropen)
        tok = lax.broadcasted_iota(jnp.int32, (nr, page), 0) % SUBLANES
        key = lax.broadcasted_iota(jnp.int32, (nr, page), 1)
        sn = jnp.where((key <= tok) & (key < tn), sn, NEG_INF)
        update(sn, [latn])
        o_lat = (acc_sc[...] / l_sc[...]).astype(BF16)
        for hh in range(H_A):
            o_ref[:, hh * V_DIM:(hh + 1) * V_DIM] = _nn(o_lat[hh * SUBLANES:(hh + 1) * SUBLANES, :], wuv_ref[hh])


def _sample_mla(page_table, qlat, qrope, lat_new, rope_new, wuv16, c_lat, c_rope, pg, tn):
    nb, n_pages = page_table.shape
    page = c_lat.shape[1]
    npg = n_pages // pg
    nr = qlat.shape[1]
    seq3 = lambda b, j, pt: (b, 0, 0)

    def page_spec(width, i):
        return pl.BlockSpec((None, page, width), lambda b, j, pt: (pt[b, j * pg + i], 0, 0))

    grid_spec = pltpu.PrefetchScalarGridSpec(
        num_scalar_prefetch=1,
        grid=(nb, npg),
        in_specs=[pl.BlockSpec((None, nr, KV_RANK), seq3),
                  pl.BlockSpec((None, nr, ROPE_DIM), seq3),
                  pl.BlockSpec((None, SUBLANES, KV_RANK), seq3),
                  pl.BlockSpec((None, SUBLANES, ROPE_DIM), seq3),
                  pl.BlockSpec(wuv16.shape, lambda b, j, pt: (0, 0, 0))]
                 + [page_spec(KV_RANK, i) for i in range(pg)]
                 + [page_spec(ROPE_DIM, i) for i in range(pg)],
        out_specs=pl.BlockSpec((None, SUBLANES, H_A * V_DIM), seq3),
        scratch_shapes=[pltpu.VMEM((nr, 1), F32), pltpu.VMEM((nr, 1), F32), pltpu.VMEM((nr, KV_RANK), F32)],
    )
    return pl.pallas_call(
        functools.partial(_smla_body, pg=pg, page=page, tn=tn),
        out_shape=jax.ShapeDtypeStruct((nb, SUBLANES, H_A * V_DIM), F32),
        grid_spec=grid_spec,
        compiler_params=_cparams(("arbitrary", "arbitrary")),
        name="sample_mla",
    )(page_table, qlat, qrope, lat_new, rope_new, wuv16, *([c_lat] * pg), *([c_rope] * pg))


def _sidx_body(pt_ref, qi_ref, wi_ref, kin_ref, *rest, pg, page, tn, k_sel):
    ik_refs = rest[:pg]
    o_ref, key_sc = rest[pg:]
    j = pl.program_id(1)
    nr = qi_ref.shape[0]
    qi = qi_ref[...]
    wi = wi_ref[...]

    def head_sum(d):
        sc = wi * jnp.maximum(d, 0.0)
        acc = sc[0:SUBLANES]
        for hh in range(1, H_IDX):
            acc = acc + sc[hh * SUBLANES:(hh + 1) * SUBLANES]
        return acc

    kc = jnp.concatenate([r[...].astype(BF16) for r in ik_refs], axis=0)
    key_sc[j] = _sort_key(head_sum(_nt(qi, kc)))

    @pl.when(j == pl.num_programs(1) - 1)
    def _():
        w = pg * page
        kn = _pad_rows(kin_ref[...], w).astype(BF16)
        sn = head_sum(_nt(qi, kn))
        tok = lax.broadcasted_iota(jnp.int32, (SUBLANES, w), 0)
        key = lax.broadcasted_iota(jnp.int32, (SUBLANES, w), 1)
        sn = jnp.where((key <= tok) & (key < tn), sn, NEG_INF)
        n_ch = key_sc.shape[0]
        key_sc[n_ch - 1] = _sort_key(sn)
        _select_topk(key_sc, n_ch, k_sel)
        for c in range(n_ch):
            o_ref[c] = pltpu.bitcast(key_sc[c], F32)


def _sample_index_mask(page_table, qi_rows, wi_rows, ki_new, c_ik, pg, tn, k_sel):
    nb, n_pages = page_table.shape
    page = c_ik.shape[1]
    npg = n_pages // pg
    n_ch = npg + 1
    w = pg * page
    nr = qi_rows.shape[1]
    seq3 = lambda b, j, pt: (b, 0, 0)
    grid_spec = pltpu.PrefetchScalarGridSpec(
        num_scalar_prefetch=1,
        grid=(nb, npg),
        in_specs=[pl.BlockSpec((None, nr, D_IDX), seq3),
                  pl.BlockSpec((None, nr, 1), seq3),
                  pl.BlockSpec((None, SUBLANES, D_IDX), seq3)]
                 + [pl.BlockSpec((None, page, D_IDX), functools.partial(
                     lambda b, j, pt, i: (pt[b, j * pg + i], 0, 0), i=i)) for i in range(pg)],
        out_specs=pl.BlockSpec((None, n_ch, SUBLANES, w), lambda b, j, pt: (b, 0, 0, 0)),
        scratch_shapes=[pltpu.VMEM((n_ch, SUBLANES, w), jnp.int32)],
    )
    return pl.pallas_call(
        functools.partial(_sidx_body, pg=pg, page=page, tn=tn, k_sel=k_sel),
        out_shape=jax.ShapeDtypeStruct((nb, n_ch, SUBLANES, w), F32),
        grid_spec=grid_spec,
        compiler_params=_cparams(("arbitrary", "arbitrary")),
        name="sample_index_mask",
    )(page_table, qi_rows, wi_rows, ki_new, *([c_ik] * pg))


def _sdsa_body(pt_ref, q_ref, mask_ref, maskn_ref, kn_ref, vn_ref, dl_ref, *rest, pg, page):
    k_refs = rest[:pg]
    v_refs = rest[pg:2 * pg]
    o_ref, m_sc, l_sc, acc_sc = rest[2 * pg:]
    j = pl.program_id(1)
    last = pl.num_programs(1) - 1
    nr = q_ref.shape[0]
    q = q_ref[...]

    @pl.when(j == 0)
    def _():
        m_sc[...] = jnp.full(m_sc.shape, M_INIT, F32)
        l_sc[...] = jnp.zeros(l_sc.shape, F32)
        acc_sc[...] = jnp.zeros(acc_sc.shape, F32)

    def rows_of(m):
        return jnp.concatenate([m] * (nr // SUBLANES), axis=0)

    def update(s, vals, width):
        m_old = m_sc[...]
        m_new = jnp.maximum(m_old, jnp.max(s, axis=-1, keepdims=True))
        alpha = jnp.exp(m_old - m_new)
        pr = jnp.exp(s - m_new)
        l_sc[...] = alpha * l_sc[...] + jnp.sum(pr, axis=-1, keepdims=True)
        acc = alpha * acc_sc[...]
        for i, v in enumerate(vals):
            acc = acc + _nn(pr[:, i * width:(i + 1) * width].astype(BF16), v)
        acc_sc[...] = acc
        m_sc[...] = m_new

    ks = [r[...].astype(BF16) for r in k_refs]
    vs = [r[...].astype(BF16) for r in v_refs]
    s = jnp.concatenate([_nt(q, kk) for kk in ks], axis=1) + rows_of(mask_ref[...])
    near = jnp.concatenate([jnp.zeros((nr, (pg - 1) * page), F32), dl_ref[:, 0:page]], axis=1)
    s = s + jnp.where(j == last, near, 0.0)
    update(s, vs, page)

    @pl.when(j == last)
    def _():
        kn = _pad_rows(kn_ref[...], page).astype(BF16)
        vn = _pad_rows(vn_ref[...], page).astype(BF16)
        sn = _nt(q, kn) + rows_of(maskn_ref[:, 0:page]) + dl_ref[:, page:2 * page]
        update(sn, [vn], page)
        res = acc_sc[...] / l_sc[...]
        lane_head = lax.broadcasted_iota(jnp.int32, (SUBLANES, H_B * DH_B), 1) // DH_B
        out = jnp.zeros((SUBLANES, H_B * DH_B), F32)
        for hh in range(H_B):
            out = jnp.where(lane_head == hh, res[hh * SUBLANES:(hh + 1) * SUBLANES, :], out)
        o_ref[...] = out


def _sample_dsa(page_table, q_bd, mask, k_new, v_new, delta_s, c_dk, c_dv, pg):
    nb, n_pages = page_table.shape
    page = c_dk.shape[1]
    npg = n_pages // pg
    w = pg * page
    nr = q_bd.shape[1]
    width = H_B * DH_B
    seq3 = lambda b, j, pt: (b, 0, 0)

    def page_spec(i):
        return pl.BlockSpec((None, page, width), lambda b, j, pt: (pt[b, j * pg + i], 0, 0))

    grid_spec = pltpu.PrefetchScalarGridSpec(
        num_scalar_prefetch=1,
        grid=(nb, npg),
        in_specs=[pl.BlockSpec((None, nr, width), seq3),
                  pl.BlockSpec((None, None, SUBLANES, w), lambda b, j, pt: (b, j, 0, 0)),
                  pl.BlockSpec((None, None, SUBLANES, w), lambda b, j, pt: (b, npg, 0, 0)),
                  pl.BlockSpec((None, SUBLANES, width), seq3),
                  pl.BlockSpec((None, SUBLANES, width), seq3),
                  pl.BlockSpec(delta_s.shape, lambda b, j, pt: (0, 0))]
                 + [page_spec(i) for i in range(pg)] + [page_spec(i) for i in range(pg)],
        out_specs=pl.BlockSpec((None, SUBLANES, width), seq3),
        scratch_shapes=[pltpu.VMEM((nr, 1), F32), pltpu.VMEM((nr, 1), F32), pltpu.VMEM((nr, width), F32)],
    )
    return pl.pallas_call(
        functools.partial(_sdsa_body, pg=pg, page=page),
        out_shape=jax.ShapeDtypeStruct((nb, SUBLANES, width), F32),
        grid_spec=grid_spec,
        compiler_params=_cparams(("arbitrary", "arbitrary")),
        name="sample_dsa",
    )(page_table, q_bd, mask, mask, k_new, v_new, delta_s, *([c_dk] * pg), *([c_dv] * pg))


def _tail_body(x_ref, oa_ref, ob_ref, gate_ref, shift_ref, scale_ref, g2_ref, woa_ref, wob_ref, wpq_ref,
               x1_ref, h2_ref, pq_ref):
    d = x_ref.shape[-1]
    mix = _nn(oa_ref[...].astype(BF16), woa_ref[...]) + _nn(ob_ref[...].astype(BF16), wob_ref[...])
    x1 = x_ref[...] + gate_ref[...] * mix
    x1_ref[...] = x1
    h2 = (_rms(x1, d) * g2_ref[...] * (1.0 + scale_ref[...]) + shift_ref[...]).astype(BF16)
    h2_ref[...] = h2
    pq_ref[...] = _nn(h2, wpq_ref[...]).astype(BF16)


def _tail(x2, o_a, o_b, gate3, shift3, scale3, mod_map, w, t):
    n, d = x2.shape
    r = gate3.shape[1]
    row = lambda i: (i, 0)
    mod_spec = pl.BlockSpec((None, r, d), mod_map)
    weights = [w["g2"], w["woa"], w["wob"], w["wpq"]]
    n_pq = w["wpq"].shape[1]
    return pl.pallas_call(
        _tail_body,
        out_shape=[jax.ShapeDtypeStruct((n, d), F32), jax.ShapeDtypeStruct((n, d), BF16),
                   jax.ShapeDtypeStruct((n, n_pq), BF16)],
        grid=(n // t,),
        in_specs=[pl.BlockSpec((t, d), row), pl.BlockSpec((t, o_a.shape[1]), row),
                  pl.BlockSpec((t, o_b.shape[1]), row), mod_spec, mod_spec, mod_spec]
                 + [_const_spec(a.shape) for a in weights],
        out_specs=[pl.BlockSpec((t, d), row), pl.BlockSpec((t, d), row), pl.BlockSpec((t, n_pq), row)],
        compiler_params=_cparams(("arbitrary",)),
        name="tail",
    )(x2, o_a, o_b, gate3, shift3, scale3, *weights)


_CANDS = [(i, j) for i in range(PEER_TOPK) for j in range(PEER_TOPK) if (i + 1) * (j + 1) <= PEER_TOPK]


def _peer_sel_body(pq_ref, sk_ref, a_ref, n_ref, b_ref, r_ref, vals_sc, rank_sc, s_sc):
    tt = pq_ref.shape[0]
    key_id = lax.broadcasted_iota(jnp.int32, (N_KEYS, tt), 0).astype(F32)
    for hh in range(PEER_HEADS):
        for p in range(2):
            g = 2 * hh + p
            s = _nt(sk_ref[g], pq_ref[:, g * LANES:(g + 1) * LANES])
            s_sc[p, hh] = s
            x = s
            rank = jnp.full((N_KEYS, tt), float(PEER_TOPK), F32)
            for it in range(PEER_TOPK):
                m = jnp.max(x, axis=0, keepdims=True)
                idx = jnp.min(jnp.where(x == m, key_id, float(N_KEYS)), axis=0, keepdims=True)
                sel = key_id == idx
                rank = jnp.where(sel, float(it), rank)
                x = jnp.where(sel, NEG_INF, x)
                vals_sc[p, it, hh:hh + 1, :] = m
            rank_sc[p, hh] = rank

    av = [vals_sc[0, i] for i in range(PEER_TOPK)]
    bv = [vals_sc[1, j] for j in range(PEER_TOPK)]
    val = [av[i] + bv[j] for i, j in _CANDS]
    n_row = [jnp.zeros((PEER_HEADS, tt), F32) for _ in range(PEER_TOPK)]
    ea = [jnp.exp(av[i] - av[0]) for i in range(PEER_TOPK)]
    eb = [jnp.exp(bv[j] - bv[0]) for j in range(PEER_TOPK)]
    z = jnp.zeros((PEER_HEADS, tt), F32)
    for c, (i, j) in enumerate(_CANDS):
        beat = jnp.zeros((PEER_HEADS, tt), F32)
        for c2 in range(len(_CANDS)):
            if c2 < c:
                beat = beat + jnp.where(val[c2] >= val[c], 1.0, 0.0)
            elif c2 > c:
                beat = beat + jnp.where(val[c2] > val[c], 1.0, 0.0)
        sel = beat < float(PEER_TOPK)
        n_row[i] = n_row[i] + jnp.where(sel, 1.0, 0.0)
        z = z + jnp.where(sel, ea[i] * eb[j], 0.0)
    inv_z = 1.0 / z

    for hh in range(PEER_HEADS):
        r1 = rank_sc[0, hh]
        r2 = rank_sc[1, hh]
        nsel = jnp.zeros((N_KEYS, tt), F32)
        for it in range(PEER_TOPK):
            nsel = jnp.where(r1 == float(it), n_row[it][hh:hh + 1, :], nsel)
        a_ref[hh] = jnp.where(r1 < float(PEER_TOPK), jnp.exp(s_sc[0, hh] - av[0][hh:hh + 1, :]), 0.0)
        n_ref[hh] = nsel
        b_ref[hh] = jnp.where(r2 < float(PEER_TOPK),
                              jnp.exp(s_sc[1, hh] - bv[0][hh:hh + 1, :]) * inv_z[hh:hh + 1, :], 0.0)
        r_ref[hh] = r2


def _peer_select(pq, sk16, tt):
    n = pq.shape[0]
    shp = jax.ShapeDtypeStruct((PEER_HEADS, N_KEYS, n), F32)
    spec = pl.BlockSpec((PEER_HEADS, N_KEYS, tt), lambda i: (0, 0, i))
    return pl.pallas_call(
        _peer_sel_body,
        out_shape=[shp, shp, shp, shp],
        grid=(n // tt,),
        in_specs=[pl.BlockSpec((tt, pq.shape[1]), lambda i: (i, 0)), _const_spec(sk16.shape)],
        out_specs=[spec, spec, spec, spec],
        scratch_shapes=[pltpu.VMEM((2, PEER_TOPK, PEER_HEADS, tt), F32),
                        pltpu.VMEM((2, PEER_HEADS, N_KEYS, tt), F32),
                        pltpu.VMEM((2, PEER_HEADS, N_KEYS, tt), F32)],
        compiler_params=_cparams(("arbitrary",)),
        name="peer_select",
    )(pq, sk16)


def _gelu(x):
    return 0.5 * x * (1.0 + lax.erf(x * (2.0 ** -0.5)))


def _peer_dense_body(h2_ref, u_ref, vt_ref, a_ref, n_ref, b_ref, r_ref, x1_ref, gate_ref, o_ref, acc_sc, act_sc,
                     *, rows_per_blk):
    eb = pl.program_id(1)

    @pl.when(eb == 0)
    def _():
        acc_sc[...] = jnp.zeros(acc_sc.shape, F32)

    z = _nt(u_ref[...], h2_ref[...])
    for i in range(rows_per_blk):
        wgt = jnp.zeros((N_KEYS, z.shape[1]), F32)
        for hh in range(PEER_HEADS):
            a_row = a_ref[hh, i:i + 1, :]
            n_row = n_ref[hh, i:i + 1, :]
            wgt = wgt + jnp.where(r_ref[hh] < n_row, a_row * b_ref[hh], 0.0)
        act_sc[i * N_KEYS:(i + 1) * N_KEYS, :] = (_gelu(z[i * N_KEYS:(i + 1) * N_KEYS, :]) * wgt).astype(BF16)
    acc_sc[...] += _nn(vt_ref[...], act_sc[...])

    @pl.when(eb == pl.num_programs(1) - 1)
    def _():
        o_ref[...] = x1_ref[...] + gate_ref[...] * acc_sc[...].T


def _peer_dense(h2, u16, vt16, a_t, n_t, b_t, r_t, x1, gate3, mod_map, tt):
    n, d = h2.shape
    n_exp = u16.shape[0]
    rows_per_blk = SUBLANES
    eblk = rows_per_blk * N_KEYS
    r = gate3.shape[1]
    half1 = pl.BlockSpec((PEER_HEADS, rows_per_blk, tt), lambda i, e: (0, e, i))
    half2 = pl.BlockSpec((PEER_HEADS, N_KEYS, tt), lambda i, e: (0, 0, i))
    return pl.pallas_call(
        functools.partial(_peer_dense_body, rows_per_blk=rows_per_blk),
        out_shape=jax.ShapeDtypeStruct((n, d), F32),
        grid=(n // tt, n_exp // eblk),
        in_specs=[pl.BlockSpec((tt, d), lambda i, e: (i, 0)),
                  pl.BlockSpec((eblk, d), lambda i, e: (e, 0)),
                  pl.BlockSpec((d, eblk), lambda i, e: (0, e)),
                  half1, half1, half2, half2,
                  pl.BlockSpec((tt, d), lambda i, e: (i, 0)),
                  pl.BlockSpec((None, r, d), lambda i, e: mod_map(i))],
        out_specs=pl.BlockSpec((tt, d), lambda i, e: (i, 0)),
        scratch_shapes=[pltpu.VMEM((d, tt), F32), pltpu.VMEM((eblk, tt), BF16)],
        compiler_params=_cparams(("arbitrary", "arbitrary")),
        name="peer_dense",
    )(h2, u16, vt16, a_t, n_t, b_t, r_t, x1, gate3)


def _t5_bucket(n):
    max_exact = N_BUCKETS // 2
    nf = jnp.maximum(n, 1).astype(F32)
    large = max_exact + (jnp.log(nf / max_exact) / math.log(MAX_DISTANCE / max_exact)
                         * (N_BUCKETS - max_exact)).astype(jnp.int32)
    return jnp.where(n < max_exact, n, jnp.minimum(large, N_BUCKETS - 1))


def _bias_delta(rel_bias, dist):
    d = jnp.maximum(dist, 0)
    tab = rel_bias[_t5_bucket(d)] - rel_bias[N_BUCKETS - 1]
    tab = jnp.where((dist >= 0)[..., None] & (d < MAX_DISTANCE)[..., None], tab, 0.0)
    return jnp.moveaxis(tab, -1, 0)


def _rope_tables(pos):
    half = ROPE_DIM // 2
    inv = ROPE_THETA ** (-jnp.arange(half, dtype=F32) / half)
    ang = pos.astype(F32)[:, None] * inv
    cos, sin = jnp.cos(ang), jnp.sin(ang)
    n = pos.shape[0]
    one = jnp.ones((n, NOPE_DIM), F32)
    zero = jnp.zeros((n, NOPE_DIM), F32)
    z16 = jnp.zeros((n, half), F32)
    tail = jnp.zeros((n, LANES - NOPE_DIM - ROPE_DIM), F32)
    tc = jnp.concatenate([one, cos, cos, tail + 1.0], axis=1)
    ts1 = jnp.concatenate([zero, -sin, z16, tail], axis=1)
    ts2 = jnp.concatenate([zero, z16, sin, tail], axis=1)
    return tc, ts1, ts2


def _layer_weights(w_in, g_norm1, g_norm2, g_cq, w_uq, g_qa, g_ckv, g_kr, w_uk, w_uv, g_qb, g_kb, w_out, w_pq,
                   subkeys, u, v):
    d = w_in.shape[0]
    offs = np.cumsum(IN_SPLITS)[:-1].tolist()
    cq, ckv, kr, qb, kb, vb, qi, ki, wi = jnp.split(w_in, offs, axis=1)
    qb_pad = jnp.zeros((d, H_B, LANES), F32)
    g_qb_pad = jnp.zeros((H_B, LANES), F32)
    qb3 = qb.reshape(d, H_B, DH_B)
    for hh in range(H_B):
        lo = (hh % 2) * DH_B
        qb_pad = qb_pad.at[:, hh, lo:lo + DH_B].set(qb3[:, hh])
        g_qb_pad = g_qb_pad.at[hh, lo:lo + DH_B].set(g_qb)
    misc = jnp.zeros((d, LANES), F32)
    misc = misc.at[:, M_KI:M_KI + D_IDX].set(ki).at[:, M_WI:M_WI + H_IDX].set(wi).at[:, M_KR:M_KR + ROPE_DIM].set(kr)
    win = jnp.concatenate([cq, ckv, qb_pad.reshape(d, H_B * LANES), kb, vb, qi, misc, jnp.tile(ki, (1, 4))], axis=1)
    dqk = NOPE_DIM + ROPE_DIM
    pad = LANES - dqk
    wuq = jnp.pad(w_uq.reshape(Q_RANK, H_A, dqk), ((0, 0), (0, 0), (0, pad))).reshape(Q_RANK, H_A * LANES)
    gqa = jnp.pad(g_qa, (0, pad)).reshape(1, LANES)
    gkr = jnp.zeros((1, LANES), F32).at[0, M_KR:M_KR + ROPE_DIM].set(g_kr)
    wk = jnp.pad(jnp.transpose(w_uk, (2, 0, 1)), ((0, 0), (0, 0), (0, LANES - NOPE_DIM))).reshape(KV_RANK, H_A * LANES)
    rpl = jnp.zeros((LANES, H_A, LANES), F32)
    eye = jnp.eye(ROPE_DIM, dtype=F32)
    for hh in range(H_A):
        rpl = rpl.at[M_KR:M_KR + ROPE_DIM, hh, NOPE_DIM:NOPE_DIM + ROPE_DIM].set(eye)
    wv = jnp.transpose(w_uv, (1, 0, 2)).reshape(KV_RANK, H_A * V_DIM)
    wukp = jnp.pad(w_uk, ((0, 0), (0, LANES - NOPE_DIM), (0, 0)))
    half = H_A * V_DIM
    return {
        "g1": g_norm1.reshape(1, d), "g2": g_norm2.reshape(1, d), "win": win.astype(BF16),
        "gcq": g_cq.reshape(1, Q_RANK), "wuq": wuq.astype(BF16), "gqa": gqa,
        "gckv": g_ckv.reshape(1, KV_RANK), "gkr": gkr, "wk": wk.astype(BF16),
        "rpl": rpl.reshape(LANES, H_A * LANES).astype(BF16), "wv": wv.astype(BF16),
        "gqb": g_qb_pad.reshape(1, H_B * LANES), "gkb": jnp.tile(g_kb, H_B).reshape(1, H_B * DH_B),
        "wukp": wukp.astype(BF16), "wuv": w_uv.astype(BF16),
        "woa": w_out[:half].astype(BF16), "wob": w_out[half:].astype(BF16), "wpq": w_pq.astype(BF16),
        "sk": subkeys.reshape(2 * PEER_HEADS, N_KEYS, PEER_DK // 2).astype(BF16),
        "u": u.astype(BF16), "vt": v.T.astype(BF16),
    }


def _head_rows(x, nb, tn, heads, width):
    x = x.reshape(nb, tn, heads, width).transpose(0, 2, 1, 3)
    x = jnp.pad(x, ((0, 0), (0, 0), (0, SUBLANES - tn), (0, 0)))
    return x.reshape(nb, heads * SUBLANES, width)


def _pad_tokens(x, nb, tn):
    return jnp.pad(x.reshape(nb, tn, x.shape[-1]), ((0, 0), (0, SUBLANES - tn), (0, 0)))


def _peer(h2, pq, x1, gate3, mod_map_of, w, t_sel, t_dense):
    a_t, n_t, b_t, r_t = _peer_select(pq, w["sk"], t_sel)
    return _peer_dense(h2, w["u"], w["vt"], a_t, n_t, b_t, r_t, x1, gate3, mod_map_of(t_dense), t_dense)


def kernel(x_prompt, x_sample, cache_mla_latent, cache_mla_rope, cache_dsa_k, cache_dsa_v, cache_idx_k, page_table, c_prompt, c_sample, w_ada, b_ada, g_norm1, g_norm2, w_in, g_cq, w_uq, g_qa, g_ckv, g_kr, w_uk, w_uv, g_qb, g_kb, rel_bias, w_out, w_pq, peer_subkeys, peer_u, peer_v):
    b, s, d = x_prompt.shape
    nb, tn, _ = x_sample.shape
    depth = w_in.shape[0]
    n_pages = page_table.shape[1]
    page = cache_mla_latent.shape[2]
    past = n_pages * page
    k_p = min(TOPK_MAX, s // 4)
    k_s = min(TOPK_MAX, (past + tn) // 4)
    n_p, n_s = b * s, nb * tn
    assert tn <= SUBLANES and past >= MAX_DISTANCE

    t_tok = min(256, s)
    t_tok_s = min(256, n_s)
    t_mla = min(512, s)
    t_dsa = min(256, s)
    t_sel, t_sel_s = min(128, n_p), min(128, n_s)
    t_dense, t_dense_s = min(512, n_p), min(512, n_s)
    pg = min(8, n_pages)
    assert n_pages % pg == 0 and t_dsa >= MAX_DISTANCE

    pos_p = jnp.arange(s, dtype=jnp.int32)
    pos_s = past + jnp.arange(tn, dtype=jnp.int32)
    tabs_p = _rope_tables(pos_p)
    tabs_s = tuple(jnp.tile(tb, (nb, 1)) for tb in _rope_tables(pos_s))

    qi_ = jnp.arange(t_dsa, dtype=jnp.int32)
    delta_p = jnp.stack([_bias_delta(rel_bias, qi_[:, None] - qi_[None, :]),
                         _bias_delta(rel_bias, qi_[:, None] - qi_[None, :] + t_dsa)], axis=1)
    tok = jnp.arange(SUBLANES, dtype=jnp.int32)
    kpos = jnp.arange(2 * page, dtype=jnp.int32) + (past - page)
    ds_ = _bias_delta(rel_bias, (past + tok)[:, None] - kpos[None, :])
    delta_s = ds_.reshape(H_B * SUBLANES, 2 * page)

    xp = x_prompt.reshape(n_p, d)
    xs = x_sample.reshape(n_s, d)
    rows_p = [[] for _ in range(5)]
    rows_s = [[] for _ in range(5)]
    c_all = jnp.concatenate([c_prompt, c_sample], axis=0)
    n_c = -(-c_all.shape[0] // SUBLANES) * SUBLANES
    c_all = jnp.pad(c_all, ((0, n_c - c_all.shape[0]), (0, 0)))

    def mod_map_p(t):
        return lambda i: (i // (s // t), 0, 0)

    def mod_map_s(t):
        return lambda i: (i, 0, 0)

    for l in range(depth):
        w = _layer_weights(w_in[l], g_norm1[l], g_norm2[l], g_cq[l], w_uq[l], g_qa[l], g_ckv[l], g_kr[l], w_uk[l],
                           w_uv[l], g_qb[l], g_kb[l], w_out[l], w_pq[l], peer_subkeys[l], peer_u[l], peer_v[l])
        mods = _ada_mods(c_all, w_ada[l], b_ada[l])
        mp = [m.reshape(b, 1, d) for m in jnp.split(mods[:b], 6, axis=-1)]
        ms_tok = [jnp.repeat(m, tn, axis=0) for m in jnp.split(mods[b:b + nb], 6, axis=-1)]

        def s_tiles(m, t):
            return m.reshape(n_s // t, t, d)

        fp = _front(xp, mp[0], mp[1], mod_map_p(t_tok), tabs_p, lambda i: (i % (s // t_tok), 0), w, t_tok)
        fs = _front(xs, s_tiles(ms_tok[0], t_tok_s), s_tiles(ms_tok[1], t_tok_s), mod_map_s(t_tok_s), tabs_s,
                    lambda i: (i, 0), w, t_tok_s)
        lat_p, kbn_p, vbf_p, misc_p, qcat_p, kcat_p, vcat_p, qbp_p, kb16_p, vb16_p, qi_p, ki4_p = fp
        lat_s, kbn_s, vbf_s, misc_s, qcat_s, _, _, qbp_s, _, _, qi_s, _ = fs

        oa_p = _mla_prompt(qcat_p, kcat_p, vcat_p, b, s, t_mla)
        ob_p = _dsa_prompt(qbp_p, kb16_p, vb16_p, qi_p, ki4_p, misc_p, delta_p, b, s, t_dsa, k_p)

        qlat = _sample_qlat(qcat_s, w["wukp"])
        qlat = jnp.pad(qlat.reshape(H_A, nb, tn, KV_RANK).transpose(1, 0, 2, 3),
                       ((0, 0), (0, 0), (0, SUBLANES - tn), (0, 0))).reshape(nb, H_A * SUBLANES, KV_RANK)
        qrope = _head_rows(qcat_s, nb, tn, H_A, LANES)[:, :, NOPE_DIM:NOPE_DIM + ROPE_DIM]
        kr_s = misc_s[:, M_KR:M_KR + ROPE_DIM]
        ki_s = misc_s[:, M_KI:M_KI + D_IDX]
        wi_s = misc_s[:, M_WI:M_WI + H_IDX]
        oa_s = _sample_mla(page_table, qlat, qrope, _pad_tokens(lat_s, nb, tn), _pad_tokens(kr_s, nb, tn),
                           w["wuv"], cache_mla_latent[l], cache_mla_rope[l], pg, tn)
        oa_s = oa_s[:, :tn].reshape(n_s, H_A * V_DIM)

        qi_rows = _head_rows(qi_s, nb, tn, H_IDX, D_IDX)
        wi_rows = _head_rows(wi_s, nb, tn, H_IDX, 1)
        mask_s = _sample_index_mask(page_table, qi_rows, wi_rows, _pad_tokens(ki_s, nb, tn), cache_idx_k[l],
                                    pg, tn, k_s)
        qb_heads = qbp_s.reshape(n_s, H_B, LANES)
        qb_nat = jnp.stack([qb_heads[:, hh, (hh % 2) * DH_B:(hh % 2 + 1) * DH_B] for hh in range(H_B)], axis=1)
        q_bd = jnp.einsum("nhd,hg->nhgd", qb_nat, jnp.eye(H_B, dtype=BF16)).reshape(n_s, H_B, H_B * DH_B)
        q_bd = _head_rows(q_bd.reshape(n_s, H_B * H_B * DH_B), nb, tn, H_B, H_B * DH_B)
        width = H_B * DH_B
        ob_s = _sample_dsa(page_table, q_bd, mask_s, _pad_tokens(kbn_s, nb, tn), _pad_tokens(vbf_s, nb, tn),
                           delta_s, cache_dsa_k[l].reshape(-1, page, width), cache_dsa_v[l].reshape(-1, page, width), pg)
        ob_s = ob_s[:, :tn].reshape(n_s, width)

        x1_p, h2_p, pq_p = _tail(xp, oa_p, ob_p, mp[2], mp[3], mp[4], mod_map_p(t_tok), w, t_tok)
        x1_s, h2_s, pq_s = _tail(xs, oa_s, ob_s, s_tiles(ms_tok[2], t_tok_s), s_tiles(ms_tok[3], t_tok_s),
                                 s_tiles(ms_tok[4], t_tok_s), mod_map_s(t_tok_s), w, t_tok_s)
        xp = _peer(h2_p, pq_p, x1_p, mp[5], mod_map_p, w, t_sel, t_dense)
        xs = _peer(h2_s, pq_s, x1_s, s_tiles(ms_tok[5], t_dense_s), mod_map_s, w, t_sel_s, t_dense_s)

        for j, (ap, as_) in enumerate(((lat_p, lat_s), (misc_p[:, M_KR:M_KR + ROPE_DIM], kr_s), (kbn_p, kbn_s),
                                       (vbf_p, vbf_s), (misc_p[:, M_KI:M_KI + D_IDX], ki_s))):
            rows_p[j].append(ap)
            rows_s[j].append(as_)

    def stack(rows, lead, tail):
        return jnp.stack(rows).reshape((depth,) + lead + tail)

    tails = [(KV_RANK,), (ROPE_DIM,), (H_B, DH_B), (H_B, DH_B), (D_IDX,)]
    outs_p = [stack(rows_p[j], (b, s), tails[j]) for j in range(5)]
    outs_s = [stack(rows_s[j], (nb, tn), tails[j]) for j in range(5)]
    return (xp.reshape(b, s, d), xs.reshape(nb, tn, d), *outs_p, *outs_s)
```
